```python
import math
import jax
import jax.numpy as jnp
from jax import lax
import numpy as np

D_MODEL = 1024
BATCH = 8
SEQ = 2048
DEPTH = 1
DEC_BATCH = 128
DEC_SEQ = 4
PAST_LEN = 16384
PAGE_SIZE = 128

S5_WIDTH = D_MODEL // 2
S5_GROUP = 16
S5_GROUPS = S5_WIDTH // S5_GROUP
S5_STATE = 64
GLA_HEADS = 4
GLA_KEY_WIDTH = D_MODEL // 4
GLA_VAL_WIDTH = D_MODEL // 2
GLA_DK = GLA_KEY_WIDTH // GLA_HEADS
GLA_DV = GLA_VAL_WIDTH // GLA_HEADS
GLA_GATE_RANK = 16
GLA_TAU = 16.0
GLA_CHUNK = 64
D_FF = 4 * D_MODEL
EPS = 1e-6

OFF_U = 0
OFF_Q = OFF_U + S5_WIDTH
OFF_K = OFF_Q + GLA_KEY_WIDTH
OFF_V = OFF_K + GLA_KEY_WIDTH
OFF_R = OFF_V + GLA_VAL_WIDTH
OFF_A = OFF_R + GLA_VAL_WIDTH
OFF_GS = OFF_A + GLA_GATE_RANK
OFF_GG = OFF_GS + D_MODEL
IN_WIDTH = OFF_GG + D_MODEL

kernel_name = "s5_gla_gated_hybrid_step"


def rmsnorm(x, g):
    xf = x.astype(jnp.float32)
    y = xf * lax.rsqrt(jnp.mean(xf * xf, axis=-1, keepdims=True) + EPS) * g.astype(jnp.float32)
    return y.astype(x.dtype)


def _complex_combine(e1, e2):
    a1r, a1i, b1r, b1i = e1
    a2r, a2i, b2r, b2i = e2
    return (a2r * a1r - a2i * a1i,
            a2r * a1i + a2i * a1r,
            a2r * b1r - a2i * b1i + b2r,
            a2r * b1i + a2i * b1r + b2i)


def s5_mixer(u, x0_re, x0_im, lam_re, lam_im, log_dt, b_re, b_im, c_re, c_im, d_skip):
    n, l, _ = u.shape
    ug = u.astype(jnp.float32).reshape(n, l, S5_GROUPS, S5_GROUP)
    dt = jnp.exp(log_dt.astype(jnp.float32))[:, None]
    lr = jnp.minimum(lam_re.astype(jnp.float32), -1e-4)
    li = lam_im.astype(jnp.float32)
    mag = jnp.exp(lr * dt)
    ar = mag * jnp.cos(li * dt)
    ai = mag * jnp.sin(li * dt)
    den = lr * lr + li * li
    fr = ((ar - 1.0) * lr + ai * li) / den
    fi = (ai * lr - (ar - 1.0) * li) / den
    br = b_re.astype(jnp.float32)
    bi = b_im.astype(jnp.float32)
    bb_r = fr[:, :, None] * br - fi[:, :, None] * bi
    bb_i = fr[:, :, None] * bi + fi[:, :, None] * br
    bu_r = jnp.einsum('nlgh,gph->nlgp', ug, bb_r)
    bu_i = jnp.einsum('nlgh,gph->nlgp', ug, bb_i)
    x0r = x0_re.astype(jnp.float32)
    x0i = x0_im.astype(jnp.float32)
    bu_r = bu_r.at[:, 0].add(ar * x0r - ai * x0i)
    bu_i = bu_i.at[:, 0].add(ar * x0i + ai * x0r)
    a_r = jnp.broadcast_to(ar, bu_r.shape)
    a_i = jnp.broadcast_to(ai, bu_i.shape)
    _, _, xs_r, xs_i = lax.associative_scan(_complex_combine, (a_r, a_i, bu_r, bu_i), axis=1)
    y = (jnp.einsum('nlgp,ghp->nlgh', xs_r, c_re.astype(jnp.float32))
         - jnp.einsum('nlgp,ghp->nlgh', xs_i, c_im.astype(jnp.float32))
         + d_skip.astype(jnp.float32) * ug)
    y = y.reshape(n, l, S5_WIDTH).astype(u.dtype)
    return y, xs_r[:, -1].astype(x0_re.dtype), xs_i[:, -1].astype(x0_im.dtype)


def gla_mixer(q, k, v, glog, s0):
    n, l = q.shape[0], q.shape[1]
    c = min(GLA_CHUNK, l)
    nc = -(-l // c)
    pad = nc * c - l

    def prep(a):
        a = jnp.pad(a.astype(jnp.float32), ((0, 0), (0, pad), (0, 0), (0, 0)))
        a = a.reshape(n, nc, c, a.shape[2], a.shape[3])
        return jnp.transpose(a, (1, 0, 3, 2, 4))

    qs, ks, vs, gs = prep(q), prep(k), prep(v), prep(glog)
    mask = jnp.tril(jnp.ones((c, c), dtype=bool))[None, None, :, :, None]

    def step(s, inp):
        qc, kc, vc, gc = inp
        b = jnp.cumsum(gc, axis=2)
        o_inter = jnp.einsum('nhcd,nhde->nhce', qc * jnp.exp(b), s)
        diff = b[:, :, :, None, :] - b[:, :, None, :, :]
        dec = jnp.exp(jnp.where(mask, diff, -jnp.inf))
        att = jnp.einsum('nhid,nhjd,nhijd->nhij', qc, kc, dec)
        o_intra = jnp.einsum('nhij,nhje->nhie', att, vc)
        b_last = b[:, :, -1:, :]
        s_new = (jnp.exp(b_last[:, :, 0, :])[..., None] * s
                 + jnp.einsum('nhcd,nhce->nhde', kc * jnp.exp(b_last - b), vc))
        return s_new, o_inter + o_intra

    s_fin, outs = lax.scan(step, s0.astype(jnp.float32), (qs, ks, vs, gs))
    o = jnp.transpose(outs, (1, 0, 3, 2, 4)).reshape(n, nc * c, GLA_HEADS, GLA_DV)[:, :l]
    return o, s_fin.astype(s0.dtype)


def hybrid_layer(x, s5_re0, s5_im0, gla_s0,
                 g_mix, w_in, lam_re, lam_im, log_dt, b_re, b_im, c_re, c_im, d_skip,
                 w_s5_glu, w_gate_up, b_gate, g_gla_norm, w_s5_br, w_gla_br, w_out,
                 g_ffn, w_ff1, w_ff2, g_final):
    n, l, _ = x.shape
    h = rmsnorm(x, g_mix)
    z = h @ w_in
    u = z[..., OFF_U:OFF_Q]
    q = z[..., OFF_Q:OFF_K].reshape(n, l, GLA_HEADS, GLA_DK) * (GLA_DK ** -0.5)
    k = z[..., OFF_K:OFF_V].reshape(n, l, GLA_HEADS, GLA_DK)
    v = z[..., OFF_V:OFF_R].reshape(n, l, GLA_HEADS, GLA_DV)
    r = z[..., OFF_R:OFF_A]
    a_low = z[..., OFF_A:OFF_GS]
    gate_s5 = z[..., OFF_GS:OFF_GG]
    gate_gla = z[..., OFF_GG:IN_WIDTH]

    y5, s5_re, s5_im = s5_mixer(u, s5_re0, s5_im0, lam_re, lam_im, log_dt,
                                b_re, b_im, c_re, c_im, d_skip)
    z5 = jax.nn.gelu(y5)
    o5 = z5 * jax.nn.sigmoid(z5 @ w_s5_glu)

    glog = (jax.nn.log_sigmoid((a_low @ w_gate_up + b_gate).astype(jnp.float32)) / GLA_TAU)
    glog = glog.reshape(n, l, GLA_HEADS, GLA_DK)
    og, gla_s = gla_mixer(q, k, v, glog, gla_s0)
    og = og * lax.rsqrt(jnp.mean(og * og, axis=-1, keepdims=True) + EPS) * g_gla_norm.astype(jnp.float32)
    og = og.reshape(n, l, GLA_VAL_WIDTH).astype(x.dtype) * jax.nn.silu(r)

    merged = jax.nn.sigmoid(gate_s5) * (o5 @ w_s5_br) + jax.nn.sigmoid(gate_gla) * (og @ w_gla_br)
    x = x + merged @ w_out

    h2 = rmsnorm(x, g_ffn)
    x = x + jnp.square(jax.nn.relu(h2 @ w_ff1)) @ w_ff2
    return rmsnorm(x, g_final), s5_re, s5_im, gla_s


def setup_inputs(seed: int = 0) -> dict:
    key = jax.random.key(seed)
    ks = jax.random.split(key, 32)
    f32 = jnp.float32
    nrm = lambda i, shape, scale: jax.random.normal(ks[i], shape, f32) * scale
    n_idx = jnp.arange(S5_STATE, dtype=f32)[None, :]
    return {
        "x_prompt": nrm(0, (BATCH, SEQ, D_MODEL), 1.0),
        "x_sample": nrm(1, (DEC_BATCH, DEC_SEQ, D_MODEL), 1.0),
        "state_s5_re": nrm(2, (DEC_BATCH, S5_GROUPS, S5_STATE), 1.0),
        "state_s5_im": nrm(3, (DEC_BATCH, S5_GROUPS, S5_STATE), 1.0),
        "state_gla": nrm(4, (DEC_BATCH, GLA_HEADS, GLA_DK, GLA_DV), 0.5),
        "g_mix": 1.0 + nrm(5, (D_MODEL,), 0.02),
        "w_in": nrm(6, (D_MODEL, IN_WIDTH), D_MODEL ** -0.5),
        "lam_re": -0.5 + nrm(7, (S5_GROUPS, S5_STATE), 0.01),
        "lam_im": jnp.pi * n_idx + nrm(8, (S5_GROUPS, S5_STATE), 0.01),
        "log_dt": jax.random.uniform(ks[9], (S5_GROUPS,), f32, math.log(1e-3), math.log(1e-1)),
        "b_re": nrm(10, (S5_GROUPS, S5_STATE, S5_GROUP), (2.0 * S5_GROUP) ** -0.5),
        "b_im": nrm(11, (S5_GROUPS, S5_STATE, S5_GROUP), (2.0 * S5_GROUP) ** -0.5),
        "c_re": nrm(12, (S5_GROUPS, S5_GROUP, S5_STATE), (2.0 * S5_STATE) ** -0.5),
        "c_im": nrm(13, (S5_GROUPS, S5_GROUP, S5_STATE), (2.0 * S5_STATE) ** -0.5),
        "d_skip": nrm(14, (S5_GROUPS, S5_GROUP), 1.0),
        "w_s5_glu": nrm(15, (S5_WIDTH, S5_WIDTH), S5_WIDTH ** -0.5),
        "w_gate_up": nrm(16, (GLA_GATE_RANK, GLA_KEY_WIDTH), GLA_GATE_RANK ** -0.5),
        "b_gate": nrm(17, (GLA_KEY_WIDTH,), 0.01),
        "g_gla_norm": 1.0 + nrm(18, (GLA_DV,), 0.02),
        "w_s5_br": nrm(19, (S5_WIDTH, D_MODEL), S5_WIDTH ** -0.5),
        "w_gla_br": nrm(20, (GLA_VAL_WIDTH, D_MODEL), GLA_VAL_WIDTH ** -0.5),
        "w_out": nrm(21, (D_MODEL, D_MODEL), D_MODEL ** -0.5),
        "g_ffn": 1.0 + nrm(22, (D_MODEL,), 0.02),
        "w_ff1": nrm(23, (D_MODEL, D_FF), D_MODEL ** -0.5),
        "w_ff2": nrm(24, (D_FF, D_MODEL), D_FF ** -0.5),
        "g_final": 1.0 + nrm(25, (D_MODEL,), 0.02),
    }


def reference(x_prompt, x_sample, state_s5_re, state_s5_im, state_gla,
              g_mix, w_in, lam_re, lam_im, log_dt, b_re, b_im, c_re, c_im, d_skip,
              w_s5_glu, w_gate_up, b_gate, g_gla_norm, w_s5_br, w_gla_br, w_out,
              g_ffn, w_ff1, w_ff2, g_final):
    weights = (g_mix, w_in, lam_re, lam_im, log_dt, b_re, b_im, c_re, c_im, d_skip,
               w_s5_glu, w_gate_up, b_gate, g_gla_norm, w_s5_br, w_gla_br, w_out,
               g_ffn, w_ff1, w_ff2, g_final)
    p_re0 = jnp.zeros((x_prompt.shape[0], S5_GROUPS, S5_STATE), state_s5_re.dtype)
    p_im0 = jnp.zeros((x_prompt.shape[0], S5_GROUPS, S5_STATE), state_s5_im.dtype)
    p_gla0 = jnp.zeros((x_prompt.shape[0], GLA_HEADS, GLA_DK, GLA_DV), state_gla.dtype)
    y_prompt = x_prompt
    y_sample = x_sample
    for _ in range(DEPTH):
        y_prompt, p_re, p_im, p_gla = hybrid_layer(y_prompt, p_re0, p_im0, p_gla0, *weights)
        y_sample, s_re, s_im, s_gla = hybrid_layer(y_sample, state_s5_re, state_s5_im, state_gla, *weights)
    return (y_prompt, y_sample, p_re, p_im, p_gla, s_re, s_im, s_gla)
```

```python
import functools
import math

import jax
import jax.numpy as jnp
import numpy as np
from jax import lax
from jax.experimental import pallas as pl
from jax.experimental.pallas import tpu as pltpu

F32 = jnp.float32
BF16 = jnp.bfloat16

D_MODEL = 1024
S5_WIDTH = 512
S5_GROUP = 16
S5_GROUPS = 32
S5_STATE = 64
S5_LANES = S5_GROUPS * S5_STATE
GLA_HEADS = 4
GLA_DK = 64
GLA_DV = 128
GLA_KEY_WIDTH = GLA_HEADS * GLA_DK
GLA_VAL_WIDTH = GLA_HEADS * GLA_DV
GLA_GATE_RANK = 16
GLA_TAU = 16.0
D_FF = 4096
EPS = 1e-6

C_U, C_Q, C_K, C_V, C_R, C_GS, C_GG, C_A, C_END = 0, 512, 768, 1024, 1536, 2048, 3072, 4096, 4224
A_PAD = C_END - C_A

LANES = 128
SUBLANES = 8
VMEM_LIMIT_BYTES = 56 * 1024 * 1024

SCAN_LANES = 512
DIAG_ROWS = 128


def _dot(a, b):
    return jnp.dot(a, b, preferred_element_type=F32)


def _split_bf16(x):
    hi = x.astype(BF16)
    lo = (x - hi.astype(F32)).astype(BF16)
    return hi, lo


def _mixer_kernel(
        x_ref, s5r0_ref, s5i0_ref, gla0_ref,
        gmix_ref, w_ref, wgu_ref, bgate_ref, bbd_ref, cbd_ref, ar_ref, ai_ref, dskip_ref,
        pin_ref, pout_ref, tri_ref, seg_ref, wglu_ref, ggn_ref, ws5_ref, wgla_ref, wout_ref,
        x1_ref, s5r_ref, s5i_ref, gla_ref,
        u_scr, bu_scr, sr_scr, si_scr, st_scr, b_scr, q_scr, k_scr, v_scr, og_scr,
        qt_scr, kt_scr, be_scr,
        *, ns, tc, tv, cg):
    rows = ns * tc
    rows_t = ns * tv
    j = pl.program_id(1)

    @pl.when(j == 0)
    def _load_state():
        sr_scr[...] = s5r0_ref[...]
        si_scr[...] = s5i0_ref[...]

        def load(i, _):
            st_scr[i] = gla0_ref[i].T
            return 0
        lax.fori_loop(0, ns * GLA_HEADS, load, 0)

    x = x_ref[...].reshape(rows, D_MODEL)
    ms = jnp.mean(x * x, axis=-1, keepdims=True)
    h = (x * lax.rsqrt(ms + EPS) * gmix_ref[...]).astype(BF16)

    u = _dot(h, w_ref[:, C_U:C_Q])
    u_scr[...] = u
    ut = _dot(pin_ref[...], u.astype(BF16)).astype(BF16)
    half = S5_LANES // 2
    for hf in range(2):
        uh = ut[:, hf * 256:(hf + 1) * 256]
        bu_scr[:, hf * half:(hf + 1) * half] = _dot(uh, bbd_ref[hf])
        bu_scr[:, S5_LANES + hf * half:S5_LANES + (hf + 1) * half] = _dot(uh, bbd_ref[2 + hf])

    for ct in range(S5_LANES // SCAN_LANES):
        lo_ = ct * SCAN_LANES
        ar = jnp.broadcast_to(ar_ref[:, lo_:lo_ + SCAN_LANES], (SUBLANES, SCAN_LANES))
        ai = jnp.broadcast_to(ai_ref[:, lo_:lo_ + SCAN_LANES], (SUBLANES, SCAN_LANES))

        def seq_tile(jt, _, lo_=lo_, ar=ar, ai=ai):
            s0 = pl.multiple_of(jt * SUBLANES, SUBLANES)

            def step(t, carry):
                xr, xi = carry
                r0 = pl.multiple_of(t * ns + jt * SUBLANES, SUBLANES)
                br = bu_scr[pl.ds(r0, SUBLANES), lo_:lo_ + SCAN_LANES]
                bi = bu_scr[pl.ds(r0, SUBLANES), S5_LANES + lo_:S5_LANES + lo_ + SCAN_LANES]
                nxr = ar * xr - ai * xi + br
                nxi = ar * xi + ai * xr + bi
                bu_scr[pl.ds(r0, SUBLANES), lo_:lo_ + SCAN_LANES] = nxr
                bu_scr[pl.ds(r0, SUBLANES), S5_LANES + lo_:S5_LANES + lo_ + SCAN_LANES] = nxi
                return nxr, nxi

            xr0 = sr_scr[pl.ds(s0, SUBLANES), lo_:lo_ + SCAN_LANES]
            xi0 = si_scr[pl.ds(s0, SUBLANES), lo_:lo_ + SCAN_LANES]
            xr, xi = lax.fori_loop(0, tv, step, (xr0, xi0), unroll=min(tv, 4))
            sr_scr[pl.ds(s0, SUBLANES), lo_:lo_ + SCAN_LANES] = xr
            si_scr[pl.ds(s0, SUBLANES), lo_:lo_ + SCAN_LANES] = xi
            return 0

        lax.fori_loop(0, ns // SUBLANES, seq_tile, 0)

    y_halves = []
    for hf in range(2):
        xcat = jnp.concatenate(
            [bu_scr[:, hf * half:(hf + 1) * half],
             bu_scr[:, S5_LANES + hf * half:S5_LANES + (hf + 1) * half]], axis=1).astype(BF16)
        y_halves.append(_dot(xcat, cbd_ref[hf]))
    yt = jnp.concatenate(y_halves, axis=1)
    yt_hi, yt_lo = _split_bf16(yt)
    y5 = _dot(pout_ref[...], yt_hi) + _dot(pout_ref[...], yt_lo)
    y5 = y5 + dskip_ref[...] * u_scr[...]
    z5 = jax.nn.gelu(y5)
    o5 = z5 * jax.nn.sigmoid(_dot(z5.astype(BF16), wglu_ref[...]))
    t5 = _dot(o5.astype(BF16), ws5_ref[...])
    gate5 = jax.nn.sigmoid(_dot(h, w_ref[:, C_GS:C_GG]))
    x1_ref[...] = (gate5 * t5).reshape(ns, tc, D_MODEL)

    q = _dot(h, w_ref[:, C_Q:C_K]) * (GLA_DK ** -0.5)
    k = _dot(h, w_ref[:, C_K:C_V])
    v_scr[...] = _dot(h, w_ref[:, C_V:C_R])
    a_low = _dot(h, w_ref[:, C_A:C_END]).astype(BF16)
    gx = _dot(a_low, wgu_ref[...]) + bgate_ref[...]
    glog = (jnp.minimum(gx, 0.0) - jnp.log(1.0 + jnp.exp(-jnp.abs(gx)))) * (1.0 / GLA_TAU)
    if tv < tc:
        step_in_seq = lax.broadcasted_iota(jnp.int32, (rows, GLA_KEY_WIDTH), 0) % tc
        glog = jnp.where(step_in_seq < tv, glog, 0.0)
    g_hi, g_lo = _split_bf16(glog)
    b_parts, e_parts = [], []
    for rb in range(rows // LANES):
        gh = g_hi[rb * LANES:(rb + 1) * LANES]
        gl = g_lo[rb * LANES:(rb + 1) * LANES]
        b_parts.append(_dot(tri_ref[...], gh) + _dot(tri_ref[...], gl))
        e_parts.append(_dot(seg_ref[...], gh) + _dot(seg_ref[...], gl))
    b = jnp.concatenate(b_parts, axis=0)
    bend = jnp.concatenate(e_parts, axis=0)
    b_scr[...] = b
    q_scr[...] = q
    k_scr[...] = k
    qt = q * jnp.exp(b)
    kt = k * jnp.exp(bend - b)
    for hh in range(GLA_HEADS):
        sl = slice(hh * GLA_DK, (hh + 1) * GLA_DK)
        qt_scr[hh] = qt[:, sl]
        kt_scr[hh] = kt[:, sl]
        be_scr[hh] = bend[:, sl]

    ncg = DIAG_ROWS // cg
    pos = lax.broadcasted_iota(jnp.int32, (DIAG_ROWS, LANES), 0) % cg
    head_sum = (lax.broadcasted_iota(jnp.int32, (LANES, 2 * GLA_DV), 0) // GLA_DK
                == lax.broadcasted_iota(jnp.int32, (LANES, 2 * GLA_DV), 1) // GLA_DV).astype(BF16)

    def diag_block(rb, _):
        r0 = pl.multiple_of(rb * DIAG_ROWS, DIAG_ROWS)
        for hp in range(GLA_HEADS // 2):
            bt = b_scr[pl.ds(r0, DIAG_ROWS), hp * LANES:(hp + 1) * LANES]
            qq = q_scr[pl.ds(r0, DIAG_ROWS), hp * LANES:(hp + 1) * LANES]
            kk = k_scr[pl.ds(r0, DIAG_ROWS), hp * LANES:(hp + 1) * LANES]
            vv = v_scr[pl.ds(r0, DIAG_ROWS), hp * 2 * GLA_DV:(hp + 1) * 2 * GLA_DV]
            b3 = bt.reshape(ncg, cg, LANES)
            k3 = kk.reshape(ncg, cg, LANES)
            v3 = vv.reshape(ncg, cg, 2 * GLA_DV)
            acc = jnp.zeros((DIAG_ROWS, 2 * GLA_DV), F32)
            for jj in range(min(cg, tv)):
                bj = jnp.broadcast_to(b3[:, jj:jj + 1, :], (ncg, cg, LANES)).reshape(DIAG_ROWS, LANES)
                kj = jnp.broadcast_to(k3[:, jj:jj + 1, :], (ncg, cg, LANES)).reshape(DIAG_ROWS, LANES)
                vj = jnp.broadcast_to(v3[:, jj:jj + 1, :], (ncg, cg, 2 * GLA_DV)).reshape(DIAG_ROWS, 2 * GLA_DV)
                dec = jnp.exp(jnp.minimum(bt - bj, 0.0))
                w = jnp.where(pos >= jj, qq * kj * dec, 0.0).astype(BF16)
                acc = acc + _dot(w, head_sum) * vj
            og_scr[pl.ds(r0, DIAG_ROWS), hp * 2 * GLA_DV:(hp + 1) * 2 * GLA_DV] = acc
        return 0

    lax.fori_loop(0, rows // DIAG_ROWS, diag_block, 0)

    def chunk_step(c, _):
        def seq_step(s, _):
            r0 = pl.multiple_of(s * tc + c * cg, cg)
            for hh in range(GLA_HEADS):
                st = st_scr[s * GLA_HEADS + hh]
                qh = qt_scr[hh, pl.ds(r0, cg), :].astype(BF16)
                o = lax.dot_general(qh, st.astype(BF16), (((1,), (1,)), ((), ())),
                                    preferred_element_type=F32)
                og_scr[pl.ds(r0, cg), hh * GLA_DV:(hh + 1) * GLA_DV] += o
                kh = kt_scr[hh, pl.ds(r0, cg), :].astype(BF16)
                vh = v_scr[pl.ds(r0, cg), hh * GLA_DV:(hh + 1) * GLA_DV].astype(BF16)
                upd = lax.dot_general(vh, kh, (((0,), (0,)), ((), ())),
                                      preferred_element_type=F32)
                decay = jnp.exp(be_scr[hh, pl.ds(r0, 1), :])
                st_scr[s * GLA_HEADS + hh] = st * decay + upd
            return 0
        lax.fori_loop(0, ns, seq_step, 0)
        return 0

    lax.fori_loop(0, tc // cg, chunk_step, 0)

    og = og_scr[...]
    r = _dot(h, w_ref[:, C_R:C_GS])
    og_parts = []
    for hh in range(GLA_HEADS):
        oh = og[:, hh * GLA_DV:(hh + 1) * GLA_DV]
        og_parts.append(oh * lax.rsqrt(jnp.mean(oh * oh, axis=-1, keepdims=True) + EPS))
    ogn = jnp.concatenate(og_parts, axis=1) * ggn_ref[...] * jax.nn.silu(r)
    tg = _dot(ogn.astype(BF16), wgla_ref[...])
    gateg = jax.nn.sigmoid(_dot(h, w_ref[:, C_GG:C_A]))
    merged = x1_ref[...].reshape(rows, D_MODEL) + gateg * tg
    out = x_ref[...].reshape(rows, D_MODEL) + _dot(merged.astype(BF16), wout_ref[...])
    x1_ref[...] = out.reshape(ns, tc, D_MODEL)

    @pl.when(j == pl.num_programs(1) - 1)
    def _store_state():
        s5r_ref[...] = sr_scr[...]
        s5i_ref[...] = si_scr[...]

        def store(i, _):
            gla_ref[i] = st_scr[i].T
            return 0
        lax.fori_loop(0, ns * GLA_HEADS, store, 0)


def _ffn_kernel(x_ref, gffn_ref, w1_ref, w2_ref, gfin_ref, y_ref, *, ff_chunk):
    x = x_ref[...]
    ms = jnp.mean(x * x, axis=-1, keepdims=True)
    h = (x * lax.rsqrt(ms + EPS) * gffn_ref[...]).astype(BF16)
    acc = x
    for c in range(D_FF // ff_chunk):
        a = _dot(h, w1_ref[:, c * ff_chunk:(c + 1) * ff_chunk])
        a = jnp.maximum(a, 0.0)
        acc = acc + _dot((a * a).astype(BF16), w2_ref[c * ff_chunk:(c + 1) * ff_chunk, :])
    ms2 = jnp.mean(acc * acc, axis=-1, keepdims=True)
    y_ref[...] = acc * lax.rsqrt(ms2 + EPS) * gfin_ref[...]


def _const_spec(shape):
    nd = len(shape)
    return pl.BlockSpec(shape, lambda *_: (0,) * nd, pipeline_mode=pl.Buffered(1))


def _perm_matrices(ns, tc, tv):
    pin = np.zeros((ns * tv, ns * tc), np.float32)
    for s in range(ns):
        for t in range(tv):
            pin[t * ns + s, s * tc + t] = 1.0
    return jnp.asarray(pin, BF16), jnp.asarray(pin.T, BF16)


def _chunk_matrices(cg):
    r = np.arange(LANES)
    same = (r[:, None] // cg) == (r[None, :] // cg)
    tri = same & (r[:, None] >= r[None, :])
    return jnp.asarray(tri, BF16), jnp.asarray(same, BF16)


def _mixer_call(x, s5r0, s5i0, gla0, weights, *, ns, tc, tv, cg):
    n, length, _ = x.shape
    rows, rows_t = ns * tc, ns * tv
    grid = (n // ns, length // tc)
    pin, pout = _perm_matrices(ns, tc, tv)
    tri, seg = _chunk_matrices(cg)
    (gmix, wcat, wgu, bgate, bbd, cbd, ar, ai, dskip, wglu, ggn, ws5, wgla, wout) = weights
    consts = (gmix, wcat, wgu, bgate, bbd, cbd, ar, ai, dskip, pin, pout, tri, seg, wglu, ggn, ws5, wgla, wout)
    in_specs = [
        pl.BlockSpec((ns, tc, D_MODEL), lambda i, j: (i, j, 0)),
        pl.BlockSpec((ns, S5_LANES), lambda i, j: (i, 0)),
        pl.BlockSpec((ns, S5_LANES), lambda i, j: (i, 0)),
        pl.BlockSpec((ns * GLA_HEADS, GLA_DK, GLA_DV), lambda i, j: (i, 0, 0)),
    ] + [_const_spec(c.shape) for c in consts]
    out_specs = [
        pl.BlockSpec((ns, tc, D_MODEL), lambda i, j: (i, j, 0)),
        pl.BlockSpec((ns, S5_LANES), lambda i, j: (i, 0)),
        pl.BlockSpec((ns, S5_LANES), lambda i, j: (i, 0)),
        pl.BlockSpec((ns * GLA_HEADS, GLA_DK, GLA_DV), lambda i, j: (i, 0, 0)),
    ]
    out_shape = [
        jax.ShapeDtypeStruct((n, length, D_MODEL), F32),
        jax.ShapeDtypeStruct((n, S5_LANES), F32),
        jax.ShapeDtypeStruct((n, S5_LANES), F32),
        jax.ShapeDtypeStruct((n * GLA_HEADS, GLA_DK, GLA_DV), F32),
    ]
    scratch = [
        pltpu.VMEM((rows, S5_WIDTH), F32),
        pltpu.VMEM((rows_t, 2 * S5_LANES), F32),
        pltpu.VMEM((ns, S5_LANES), F32),
        pltpu.VMEM((ns, S5_LANES), F32),
        pltpu.VMEM((ns * GLA_HEADS, GLA_DV, GLA_DK), F32),
        pltpu.VMEM((rows, GLA_KEY_WIDTH), F32),
        pltpu.VMEM((rows, GLA_KEY_WIDTH), F32),
        pltpu.VMEM((rows, GLA_KEY_WIDTH), F32),
        pltpu.VMEM((rows, GLA_VAL_WIDTH), F32),
        pltpu.VMEM((rows, GLA_VAL_WIDTH), F32),
        pltpu.VMEM((GLA_HEADS, rows, GLA_DK), F32),
        pltpu.VMEM((GLA_HEADS, rows, GLA_DK), F32),
        pltpu.VMEM((GLA_HEADS, rows, GLA_DK), F32),
    ]
    kern = functools.partial(_mixer_kernel, ns=ns, tc=tc, tv=tv, cg=cg)
    return pl.pallas_call(
        kern,
        grid=grid,
        in_specs=in_specs,
        out_specs=out_specs,
        out_shape=out_shape,
        scratch_shapes=scratch,
        compiler_params=pltpu.CompilerParams(
            dimension_semantics=("arbitrary", "arbitrary"),
            vmem_limit_bytes=VMEM_LIMIT_BYTES),
        name=f"mixer_ns{ns}_tc{tc}",
    )(x, s5r0, s5i0, gla0, *consts)


def _ffn_call(x, gffn, w1, w2, gfin, *, tm):
    m = x.shape[0]
    kern = functools.partial(_ffn_kernel, ff_chunk=1024)
    return pl.pallas_call(
        kern,
        grid=(m // tm,),
        in_specs=[
            pl.BlockSpec((tm, D_MODEL), lambda i: (i, 0)),
            _const_spec(gffn.shape), _const_spec(w1.shape), _const_spec(w2.shape), _const_spec(gfin.shape),
        ],
        out_specs=pl.BlockSpec((tm, D_MODEL), lambda i: (i, 0)),
        out_shape=jax.ShapeDtypeStruct((m, D_MODEL), F32),
        compiler_params=pltpu.CompilerParams(
            dimension_semantics=("arbitrary",),
            vmem_limit_bytes=VMEM_LIMIT_BYTES),
        name=f"ffn_m{m}",
    )(x, gffn, w1, w2, gfin)


def _s5_discretize(lam_re, lam_im, log_dt, b_re, b_im):
    dt = jnp.exp(log_dt.astype(F32))[:, None]
    lr = jnp.minimum(lam_re.astype(F32), -1e-4)
    li = lam_im.astype(F32)
    mag = jnp.exp(lr * dt)
    ar = mag * jnp.cos(li * dt)
    ai = mag * jnp.sin(li * dt)
    den = lr * lr + li * li
    fr = ((ar - 1.0) * lr + ai * li) / den
    fi = (ai * lr - (ar - 1.0) * li) / den
    br = b_re.astype(F32)
    bi = b_im.astype(F32)
    bb_r = fr[:, :, None] * br - fi[:, :, None] * bi
    bb_i = fr[:, :, None] * bi + fi[:, :, None] * br
    return ar, ai, bb_r, bb_i


def _block_diag(blocks):
    g, a, b = blocks.shape
    eye = jnp.eye(g, dtype=blocks.dtype)
    return jnp.einsum('gab,gk->gakb', blocks, eye).reshape(g * a, g * b)


def _prepare_weights(g_mix, w_in, lam_re, lam_im, log_dt, b_re, b_im, c_re, c_im, d_skip,
                     w_s5_glu, w_gate_up, b_gate, g_gla_norm, w_s5_br, w_gla_br, w_out):
    off_a = 2048
    off_gs = off_a + GLA_GATE_RANK
    off_gg = off_gs + D_MODEL
    wa = jnp.pad(w_in[:, off_a:off_gs], ((0, 0), (0, A_PAD - GLA_GATE_RANK)))
    wcat = jnp.concatenate([w_in[:, :off_a], w_in[:, off_gs:off_gg], w_in[:, off_gg:], wa], axis=1).astype(BF16)
    wgu = jnp.pad(w_gate_up, ((0, A_PAD - GLA_GATE_RANK), (0, 0))).astype(BF16)
    ar, ai, bb_r, bb_i = _s5_discretize(lam_re, lam_im, log_dt, b_re, b_im)
    bre = _block_diag(jnp.transpose(bb_r, (0, 2, 1)))
    bim = _block_diag(jnp.transpose(bb_i, (0, 2, 1)))
    hk, hn = S5_WIDTH // 2, S5_LANES // 2
    bbd = jnp.stack([bre[:hk, :hn], bre[hk:, hn:], bim[:hk, :hn], bim[hk:, hn:]]).astype(BF16)
    cre = _block_diag(jnp.transpose(c_re.astype(F32), (0, 2, 1)))
    cim = _block_diag(jnp.transpose(c_im.astype(F32), (0, 2, 1)))
    cbd = jnp.stack([
        jnp.concatenate([cre[:hn, :hk], -cim[:hn, :hk]], axis=0),
        jnp.concatenate([cre[hn:, hk:], -cim[hn:, hk:]], axis=0)]).astype(BF16)
    return (
        g_mix.astype(F32).reshape(1, D_MODEL), wcat, wgu, b_gate.astype(F32).reshape(1, GLA_KEY_WIDTH),
        bbd, cbd, ar.reshape(1, S5_LANES), ai.reshape(1, S5_LANES),
        d_skip.astype(F32).reshape(1, S5_WIDTH), w_s5_glu.astype(BF16),
        jnp.tile(g_gla_norm.astype(F32), GLA_HEADS).reshape(1, GLA_VAL_WIDTH),
        w_s5_br.astype(BF16), w_gla_br.astype(BF16), w_out.astype(BF16))


def kernel(x_prompt, x_sample, state_s5_re, state_s5_im, state_gla, g_mix, w_in, lam_re, lam_im, log_dt, b_re, b_im, c_re, c_im, d_skip, w_s5_glu, w_gate_up, b_gate, g_gla_norm, w_s5_br, w_gla_br, w_out, g_ffn, w_ff1, w_ff2, g_final):
    weights = _prepare_weights(g_mix, w_in, lam_re, lam_im, log_dt, b_re, b_im, c_re, c_im, d_skip,
                               w_s5_glu, w_gate_up, b_gate, g_gla_norm, w_s5_br, w_gla_br, w_out)
    gffn = g_ffn.astype(F32).reshape(1, D_MODEL)
    gfin = g_final.astype(F32).reshape(1, D_MODEL)
    w1 = w_ff1.astype(BF16)
    w2 = w_ff2.astype(BF16)

    nb, seq, _ = x_prompt.shape
    nd, dseq, _ = x_sample.shape

    zeros_s5 = jnp.zeros((nb, S5_LANES), F32)
    zeros_gla = jnp.zeros((nb * GLA_HEADS, GLA_DK, GLA_DV), F32)
    x1p, p_re, p_im, p_gla = _mixer_call(x_prompt, zeros_s5, zeros_s5, zeros_gla, weights,
                                         ns=nb, tc=64, tv=64, cg=16)
    y_prompt = _ffn_call(x1p.reshape(nb * seq, D_MODEL), gffn, w1, w2, gfin, tm=1024)

    pad_t = SUBLANES
    xs = jnp.pad(x_sample, ((0, 0), (0, pad_t - dseq), (0, 0)))
    x1s, s_re, s_im, s_gla = _mixer_call(
        xs, state_s5_re.reshape(nd, S5_LANES), state_s5_im.reshape(nd, S5_LANES),
        state_gla.reshape(nd * GLA_HEADS, GLA_DK, GLA_DV), weights,
        ns=32, tc=pad_t, tv=dseq, cg=pad_t)
    y_sample = _ffn_call(x1s.reshape(nd * pad_t, D_MODEL), gffn, w1, w2, gfin, tm=nd * pad_t)
    y_sample = y_sample.reshape(nd, pad_t, D_MODEL)[:, :dseq]

    return (y_prompt.reshape(nb, seq, D_MODEL), y_sample,
            p_re.reshape(nb, S5_GROUPS, S5_STATE), p_im.reshape(nb, S5_GROUPS, S5_STATE),
            p_gla.reshape(nb, GLA_HEADS, GLA_DK, GLA_DV),
            s_re.reshape(nd, S5_GROUPS, S5_STATE), s_im.reshape(nd, S5_GROUPS, S5_STATE),
            s_gla.reshape(nd, GLA_HEADS, GLA_DK, GLA_DV))
```
